```python
import jax, jax.numpy as jnp
from jax import lax
import numpy as np

D_MODEL = 2048
BATCH = 4
SEQ = 4096
DEPTH = 4

FNET_GROUPS = 4
FNET_GROUP_DIM = D_MODEL // 8
FNET_WIDTH = FNET_GROUPS * FNET_GROUP_DIM
GLA_HEADS = 4
GLA_KEY_DIM = D_MODEL // 16
GLA_VAL_DIM = D_MODEL // 8
GLA_QK = GLA_HEADS * GLA_KEY_DIM
GLA_V = GLA_HEADS * GLA_VAL_DIM
GLA_GATE_RANK = 16
GLA_GATE_TEMP = 16.0
GLA_CHUNK = 64
EVEN_SPLITS = (FNET_WIDTH, GLA_QK, GLA_QK, GLA_V, GLA_V, GLA_GATE_RANK, GLA_GATE_RANK)
EVEN_IN_WIDTH = sum(EVEN_SPLITS)
EVEN_MIX_WIDTH = FNET_WIDTH + GLA_V
ATT_HEADS = 16
ATT_HEAD_DIM = D_MODEL // 16
ATT_WIDTH = ATT_HEADS * ATT_HEAD_DIM
DILATED_GROUPS = ((128, 1), (512, 4), (2048, 16))
ATT_BLOCK = 128
REL_BUCKETS = 32
REL_MAX_DISTANCE = 1024
D_FF = ((8 * D_MODEL // 3 + 255) // 256) * 256
CONV_WIDTH = 3
EPS = 1e-6

kernel_name = "hybrid_fourier_gla_dilated_encoder"


def rms_norm(x, g):
    xf = x.astype(jnp.float32)
    y = xf * lax.rsqrt(jnp.mean(xf * xf, axis=-1, keepdims=True) + EPS)
    return (y * g.astype(jnp.float32)).astype(x.dtype)


def t5_bucket(rel):
    nb = REL_BUCKETS // 2
    ret = (rel > 0).astype(np.int32) * nb
    n = np.abs(rel)
    max_exact = nb // 2
    large = max_exact + (np.log(np.maximum(n, 1) / max_exact)
                         / np.log(REL_MAX_DISTANCE / max_exact)
                         * (nb - max_exact)).astype(np.int32)
    large = np.minimum(large, nb - 1)
    return (ret + np.where(n < max_exact, n, large)).astype(np.int32)


def gla_scan(q, k, v, log_a):
    bsz, seq, heads, dk = q.shape
    dv = v.shape[-1]
    n_chunks = seq // GLA_CHUNK

    def to_chunks(t):
        return t.reshape(bsz, n_chunks, GLA_CHUNK, heads, t.shape[-1]).transpose(1, 0, 3, 2, 4)

    tri = jnp.tril(jnp.ones((GLA_CHUNK, GLA_CHUNK), dtype=bool))[:, :, None]

    def step(state, inp):
        qc, kc, vc, ac = inp
        b = jnp.cumsum(ac, axis=2)
        b_last = b[:, :, -1:, :]
        diff = b[:, :, :, None, :] - b[:, :, None, :, :]
        decay = jnp.exp(jnp.where(tri, diff, -jnp.inf))
        scores = jnp.einsum('bhtk,bhsk,bhtsk->bhts', qc, kc, decay)
        o = (jnp.einsum('bhts,bhsv->bhtv', scores, vc)
             + jnp.einsum('bhtk,bhkv->bhtv', qc * jnp.exp(b), state))
        new_state = (jnp.exp(b_last[:, :, 0, :])[..., None] * state
                     + jnp.einsum('bhsk,bhsv->bhkv', kc * jnp.exp(b_last - b), vc))
        return new_state, o

    state0 = jnp.zeros((bsz, heads, dk, dv), jnp.float32)
    _, o = lax.scan(step, state0, (to_chunks(q), to_chunks(k), to_chunks(v), to_chunks(log_a)))
    return o.transpose(1, 0, 3, 2, 4).reshape(bsz, seq, heads, dv)


def fourier_gla_mixer(xn, w_in, gate_up_f, gate_bias_f, gate_up_b, gate_bias_b, gla_g, w_out):
    bsz, seq, _ = xn.shape
    proj = xn @ w_in
    a, q, k, v, g, zf, zb = jnp.split(proj, np.cumsum(EVEN_SPLITS)[:-1].tolist(), axis=-1)
    a = a.reshape(bsz, seq, FNET_GROUPS, FNET_GROUP_DIM).astype(jnp.float32)
    a = jnp.fft.fft2(a, axes=(1, 3), norm='ortho').real.reshape(bsz, seq, FNET_WIDTH)
    q = q.reshape(bsz, seq, GLA_HEADS, GLA_KEY_DIM).astype(jnp.float32) * (GLA_KEY_DIM ** -0.5)
    k = k.reshape(bsz, seq, GLA_HEADS, GLA_KEY_DIM).astype(jnp.float32)
    v = v.reshape(bsz, seq, GLA_HEADS, GLA_VAL_DIM).astype(jnp.float32)
    log_af = (jax.nn.log_sigmoid((zf @ gate_up_f + gate_bias_f).astype(jnp.float32))
              / GLA_GATE_TEMP).reshape(bsz, seq, GLA_HEADS, GLA_KEY_DIM)
    log_ab = (jax.nn.log_sigmoid((zb @ gate_up_b + gate_bias_b).astype(jnp.float32))
              / GLA_GATE_TEMP).reshape(bsz, seq, GLA_HEADS, GLA_KEY_DIM)
    flip = lambda t: jnp.flip(t, axis=1)
    o = gla_scan(q, k, v, log_af) + flip(gla_scan(flip(q), flip(k), flip(v), flip(log_ab)))
    gate = jax.nn.silu(g.reshape(bsz, seq, GLA_HEADS, GLA_VAL_DIM).astype(jnp.float32))
    o = (rms_norm(o, gla_g) * gate).reshape(bsz, seq, GLA_V)
    mixed = jnp.concatenate([a, o], axis=-1).astype(xn.dtype)
    return mixed @ w_out


def dilated_branch(q, k, v, rel_bias, window, dilation):
    bsz, seq, heads, hd = q.shape
    half = window // (2 * dilation)
    sub_len = seq // dilation
    n_blk = -(-sub_len // ATT_BLOCK)
    pad_len = n_blk * ATT_BLOCK
    kb_len = ATT_BLOCK + 2 * half

    def to_sub(t):
        return t.reshape(bsz, sub_len, dilation, heads, hd).transpose(0, 2, 1, 3, 4)

    qs = jnp.pad(to_sub(q), ((0, 0), (0, 0), (0, pad_len - sub_len), (0, 0), (0, 0)))
    qs = qs.reshape(bsz, dilation, n_blk, ATT_BLOCK, heads, hd)
    key_idx = np.arange(n_blk)[:, None] * ATT_BLOCK + np.arange(kb_len)[None, :]
    kv_pad = ((0, 0), (0, 0), (half, pad_len - sub_len + half), (0, 0), (0, 0))
    ks = jnp.pad(to_sub(k), kv_pad)[:, :, key_idx]
    vs = jnp.pad(to_sub(v), kv_pad)[:, :, key_idx]

    rel = np.arange(kb_len)[None, :] - half - np.arange(ATT_BLOCK)[:, None]
    key_pos = key_idx - half
    valid = ((np.abs(rel) <= half)[None]
             & ((key_pos >= 0) & (key_pos < sub_len))[:, None, :])
    bias = rel_bias[t5_bucket(rel * dilation)].astype(jnp.float32).transpose(2, 0, 1)

    s = jnp.einsum('brnqhc,brnkhc->brnhqk', qs, ks) + bias
    s = jnp.where(valid[:, None], s, -jnp.inf)
    m = jnp.max(s, axis=-1)
    m = jnp.where(jnp.isfinite(m), m, 0.0)
    p = jnp.exp(s - m[..., None])
    l = jnp.sum(p, axis=-1)
    num = jnp.einsum('brnhqk,brnkhc->brnqhc', p, vs)

    def from_sub(t):
        t = t.reshape(bsz, dilation, pad_len, *t.shape[4:])[:, :, :sub_len]
        t = jnp.swapaxes(t, 1, 2)
        return t.reshape(bsz, seq, *t.shape[3:])

    return (from_sub(jnp.swapaxes(m, 3, 4)), from_sub(jnp.swapaxes(l, 3, 4)), from_sub(num))


def dilated_mixer(xn, w_qkv, q_norm_g, k_norm_g, rel_bias, w_out):
    bsz, seq, _ = xn.shape
    q, k, v = jnp.split(xn @ w_qkv, 3, axis=-1)
    shp = (bsz, seq, ATT_HEADS, ATT_HEAD_DIM)
    q = rms_norm(q.reshape(shp), q_norm_g).astype(jnp.float32) * (ATT_HEAD_DIM ** -0.5)
    k = rms_norm(k.reshape(shp), k_norm_g).astype(jnp.float32)
    v = v.reshape(shp).astype(jnp.float32)
    branches = [dilated_branch(q, k, v, rel_bias, w, d) for (w, d) in DILATED_GROUPS]
    ms = jnp.stack([br[0] for br in branches])
    ls = jnp.stack([br[1] for br in branches])
    nums = jnp.stack([br[2] for br in branches])
    wts = jnp.exp(ms - jnp.max(ms, axis=0, keepdims=True))
    out = jnp.sum(wts[..., None] * nums, axis=0) / jnp.sum(wts * ls, axis=0)[..., None]
    return out.reshape(bsz, seq, ATT_WIDTH).astype(xn.dtype) @ w_out


def conv_ffn(xn, w_gate, w_up, conv_w, conv_b, w_down):
    g = xn @ w_gate
    g = lax.conv_general_dilated(g, conv_w[:, None, :], window_strides=(1,),
                                 padding=[((CONV_WIDTH - 1) // 2, (CONV_WIDTH - 1) // 2)],
                                 dimension_numbers=('NWC', 'WIO', 'NWC'),
                                 feature_group_count=D_FF) + conv_b
    h = jax.nn.gelu(g) * (xn @ w_up)
    return h @ w_down


def setup_inputs(seed: int = 0) -> dict:
    key = jax.random.key(seed)
    ks = jax.random.split(key, 24)
    ne = (DEPTH + 1) // 2
    no = DEPTH // 2
    nrm = lambda k, shape, scale: jax.random.normal(k, shape, jnp.float32) * scale
    return {
        "x": nrm(ks[0], (BATCH, SEQ, D_MODEL), 1.0),
        "mix_norm_g": 1.0 + nrm(ks[1], (DEPTH, D_MODEL), 0.01),
        "w_in_even": nrm(ks[2], (ne, D_MODEL, EVEN_IN_WIDTH), D_MODEL ** -0.5),
        "gate_up_fwd": nrm(ks[3], (ne, GLA_GATE_RANK, GLA_QK), GLA_GATE_RANK ** -0.5),
        "gate_bias_fwd": nrm(ks[4], (ne, GLA_QK), 0.1),
        "gate_up_bwd": nrm(ks[5], (ne, GLA_GATE_RANK, GLA_QK), GLA_GATE_RANK ** -0.5),
        "gate_bias_bwd": nrm(ks[6], (ne, GLA_QK), 0.1),
        "gla_norm_g": 1.0 + nrm(ks[7], (ne, GLA_VAL_DIM), 0.01),
        "w_out_even": nrm(ks[8], (ne, EVEN_MIX_WIDTH, D_MODEL), EVEN_MIX_WIDTH ** -0.5),
        "w_qkv_odd": nrm(ks[9], (no, D_MODEL, 3 * ATT_WIDTH), D_MODEL ** -0.5),
        "q_norm_g": 1.0 + nrm(ks[10], (no, ATT_HEAD_DIM), 0.01),
        "k_norm_g": 1.0 + nrm(ks[11], (no, ATT_HEAD_DIM), 0.01),
        "rel_bias": nrm(ks[12], (REL_BUCKETS, ATT_HEADS), 0.1),
        "w_out_odd": nrm(ks[13], (no, ATT_WIDTH, D_MODEL), ATT_WIDTH ** -0.5),
        "ffn_norm_g": 1.0 + nrm(ks[14], (DEPTH, D_MODEL), 0.01),
        "w_gate": nrm(ks[15], (DEPTH, D_MODEL, D_FF), D_MODEL ** -0.5),
        "w_up": nrm(ks[16], (DEPTH, D_MODEL, D_FF), D_MODEL ** -0.5),
        "conv_w": nrm(ks[17], (DEPTH, CONV_WIDTH, D_FF), CONV_WIDTH ** -0.5),
        "conv_b": nrm(ks[18], (DEPTH, D_FF), 0.01),
        "w_down": nrm(ks[19], (DEPTH, D_FF, D_MODEL), D_FF ** -0.5),
    }


def reference(x, mix_norm_g, w_in_even, gate_up_fwd, gate_bias_fwd, gate_up_bwd, gate_bias_bwd,
              gla_norm_g, w_out_even, w_qkv_odd, q_norm_g, k_norm_g, rel_bias, w_out_odd,
              ffn_norm_g, w_gate, w_up, conv_w, conv_b, w_down):
    h = x
    for layer in range(DEPTH):
        i = layer // 2
        xn = rms_norm(h, mix_norm_g[layer])
        if layer % 2 == 0:
            h = h + fourier_gla_mixer(xn, w_in_even[i], gate_up_fwd[i], gate_bias_fwd[i],
                                      gate_up_bwd[i], gate_bias_bwd[i], gla_norm_g[i], w_out_even[i])
        else:
            h = h + dilated_mixer(xn, w_qkv_odd[i], q_norm_g[i], k_norm_g[i], rel_bias, w_out_odd[i])
        h = h + conv_ffn(rms_norm(h, ffn_norm_g[layer]), w_gate[layer], w_up[layer],
                         conv_w[layer], conv_b[layer], w_down[layer])
    return h
```

```python
import functools

import numpy as np
import jax
import jax.numpy as jnp
from jax import lax
from jax.experimental import pallas as pl
from jax.experimental.pallas import tpu as pltpu

F32 = jnp.float32
BF16 = jnp.bfloat16

D_MODEL = 2048
BATCH = 4
SEQ = 4096
DEPTH = 4
N_TOK = BATCH * SEQ
FNET_GROUPS = 4
FNET_GROUP_DIM = 256
FNET_WIDTH = 1024
GLA_HEADS = 4
GLA_KEY_DIM = 128
GLA_VAL_DIM = 256
GLA_QK = 512
GLA_V = 1024
GLA_GATE_RANK = 16
GLA_GATE_TEMP = 16.0
EVEN_MAIN = FNET_WIDTH + 2 * GLA_QK + 2 * GLA_V
ATT_HEADS = 16
ATT_HEAD_DIM = 128
ATT_WIDTH = 2048
DILATED_GROUPS = ((128, 1), (512, 4), (2048, 16))
ATT_BLOCK = 128
ATT_HALF = 64
REL_BUCKETS = 32
REL_MAX_DISTANCE = 1024
D_FF = 5632
EPS = 1e-6
NEG = -1e30

LANES = 128
BF16_SUBLANES = 16
MIB = 1024 * 1024

TM = 512
FFT_R = 64
GLA_CHUNK = 64
FFN_TF = 512
HALO = BF16_SUBLANES


def _cparams(sem, vmem_mib):
    return pltpu.CompilerParams(dimension_semantics=sem, vmem_limit_bytes=vmem_mib * MIB)


def _rms(x, g):
    ms = jnp.mean(x * x, axis=-1, keepdims=True)
    return x * lax.rsqrt(ms + EPS) * g


def _in_proj_even_kernel(h_ref, g_ref, w_ref, wz_ref, o_ref, z_ref, xn_ref):
    @pl.when(pl.program_id(1) == 0)
    def _():
        xn = _rms(h_ref[...], g_ref[...]).astype(BF16)
        xn_ref[...] = xn
        z_ref[...] = jnp.dot(xn, wz_ref[...], preferred_element_type=F32)

    o_ref[...] = jnp.dot(xn_ref[...], w_ref[...], preferred_element_type=F32).astype(o_ref.dtype)


def _in_proj_even(h, g, w_main, w_z):
    tn = 1024
    nz = w_z.shape[1]
    return pl.pallas_call(
        _in_proj_even_kernel,
        grid=(N_TOK // TM, EVEN_MAIN // tn),
        in_specs=[
            pl.BlockSpec((TM, D_MODEL), lambda i, j: (i, 0)),
            pl.BlockSpec((1, D_MODEL), lambda i, j: (0, 0)),
            pl.BlockSpec((D_MODEL, tn), lambda i, j: (0, j)),
            pl.BlockSpec((D_MODEL, nz), lambda i, j: (0, 0)),
        ],
        out_specs=[
            pl.BlockSpec((TM, tn), lambda i, j: (i, j)),
            pl.BlockSpec((TM, nz), lambda i, j: (i, 0)),
        ],
        out_shape=[
            jax.ShapeDtypeStruct((N_TOK, EVEN_MAIN), BF16),
            jax.ShapeDtypeStruct((N_TOK, nz), F32),
        ],
        scratch_shapes=[pltpu.VMEM((TM, D_MODEL), BF16)],
        compiler_params=_cparams(("parallel", "arbitrary"), 40),
        name="in_proj_even",
    )(h, g, w_main, w_z)


def _fft_tables():
    r = FFT_R
    idx = np.arange(r)
    ang = 2.0 * np.pi * np.outer(idx, idx) / r
    c, s = np.cos(ang), np.sin(ang)
    fa = np.concatenate([c, -s], axis=0)
    fb = np.block([[c, s], [-s, c]])
    tw = 2.0 * np.pi * np.outer(idx, idx) / (r * r)
    tr = np.repeat(np.cos(tw), LANES, axis=1)
    ti = np.repeat(-np.sin(tw), LANES, axis=1)
    cidx = np.arange(FNET_GROUP_DIM)
    cang = 2.0 * np.pi * np.outer(cidx, cidx) / FNET_GROUP_DIM
    scale = 1.0 / np.sqrt(SEQ * FNET_GROUP_DIM)
    cc = np.cos(cang) * scale
    sc = np.sin(cang) * scale
    return (jnp.asarray(fa, BF16), jnp.asarray(fb, BF16), jnp.asarray(tr, F32),
            jnp.asarray(ti, F32), jnp.asarray(cc, BF16), jnp.asarray(sc, BF16))


def _fft_a_kernel(x_ref, fa_ref, tr_ref, ti_ref, zr_ref, zi_ref, *, ns2):
    z = jnp.dot(fa_ref[...], x_ref[0], preferred_element_type=F32)
    rep = FNET_WIDTH // LANES
    for s in range(ns2):
        tr = jnp.tile(tr_ref[:, s * LANES:(s + 1) * LANES], (1, rep))
        ti = jnp.tile(ti_ref[:, s * LANES:(s + 1) * LANES], (1, rep))
        sl = slice(s * FNET_WIDTH, (s + 1) * FNET_WIDTH)
        a = z[:FFT_R, sl]
        b = z[FFT_R:, sl]
        zr_ref[0, :, sl] = (a * tr - b * ti).astype(BF16)
        zi_ref[0, :, sl] = (a * ti + b * tr).astype(BF16)


def _fft_b_kernel(zr_ref, zi_ref, fb_ref, cc_ref, sc_ref, o_ref, ur_ref, ui_ref, *, tk):
    for k1 in range(tk):
        rhs = jnp.concatenate([zr_ref[0, k1], zi_ref[0, k1]], axis=0)
        u = jnp.dot(fb_ref[...], rhs, preferred_element_type=F32)
        ur_ref[k1 * FFT_R:(k1 + 1) * FFT_R, :] = u[:FFT_R].astype(BF16)
        ui_ref[k1 * FFT_R:(k1 + 1) * FFT_R, :] = u[FFT_R:].astype(BF16)
    for g in range(FNET_GROUPS):
        sl = slice(g * FNET_GROUP_DIM, (g + 1) * FNET_GROUP_DIM)
        y = (jnp.dot(ur_ref[:, sl], cc_ref[...], preferred_element_type=F32)
             + jnp.dot(ui_ref[:, sl], sc_ref[...], preferred_element_type=F32))
        for k1 in range(tk):
            o_ref[0, k1, :, sl] = y[k1 * FFT_R:(k1 + 1) * FFT_R].astype(o_ref.dtype)


def _fourier_mix(a, tables):
    fa, fb, tr, ti, cc, sc = tables
    r = FFT_R
    ns2 = 4
    tc = ns2 * FNET_WIDTH
    x = a.reshape(BATCH, r, r * FNET_WIDTH)
    zr, zi = pl.pallas_call(
        functools.partial(_fft_a_kernel, ns2=ns2),
        grid=(BATCH, r // ns2),
        in_specs=[
            pl.BlockSpec((1, r, tc), lambda b, j: (b, 0, j)),
            pl.BlockSpec((2 * r, r), lambda b, j: (0, 0)),
            pl.BlockSpec((r, ns2 * LANES), lambda b, j: (0, j)),
            pl.BlockSpec((r, ns2 * LANES), lambda b, j: (0, j)),
        ],
        out_specs=[pl.BlockSpec((1, r, tc), lambda b, j: (b, 0, j))] * 2,
        out_shape=[jax.ShapeDtypeStruct((BATCH, r, r * FNET_WIDTH), BF16)] * 2,
        compiler_params=_cparams(("parallel", "parallel"), 32),
        name="fft_stage_a",
    )(x, fa, tr, ti)
    zr = zr.reshape(BATCH, r, r, FNET_WIDTH)
    zi = zi.reshape(BATCH, r, r, FNET_WIDTH)
    tk = 8
    return pl.pallas_call(
        functools.partial(_fft_b_kernel, tk=tk),
        grid=(BATCH, r // tk),
        in_specs=[
            pl.BlockSpec((1, tk, r, FNET_WIDTH), lambda b, j: (b, j, 0, 0)),
            pl.BlockSpec((1, tk, r, FNET_WIDTH), lambda b, j: (b, j, 0, 0)),
            pl.BlockSpec((2 * r, 2 * r), lambda b, j: (0, 0)),
            pl.BlockSpec((FNET_GROUP_DIM, FNET_GROUP_DIM), lambda b, j: (0, 0)),
            pl.BlockSpec((FNET_GROUP_DIM, FNET_GROUP_DIM), lambda b, j: (0, 0)),
        ],
        out_specs=pl.BlockSpec((1, tk, r, FNET_WIDTH), lambda b, j: (b, j, 0, 0)),
        out_shape=jax.ShapeDtypeStruct((BATCH, r, r, FNET_WIDTH), BF16),
        scratch_shapes=[pltpu.VMEM((tk * r, FNET_WIDTH), BF16)] * 2,
        compiler_params=_cparams(("parallel", "parallel"), 32),
        name="fft_stage_b",
    )(zr, zi, fb, cc, sc)


def _gla_kernel(q_ref, k_ref, v_ref, z_ref, gu_ref, gb_ref, o_ref, st_ref, *, reverse):
    c = GLA_CHUNK

    @pl.when(pl.program_id(1) == 0)
    def _():
        st_ref[...] = jnp.zeros_like(st_ref)

    z = z_ref[...]
    zd = z[:, GLA_GATE_RANK:] if reverse else z[:, :GLA_GATE_RANK]
    x = jnp.dot(zd, gu_ref[...], preferred_element_type=F32,
                precision=lax.Precision.HIGHEST) + gb_ref[...]
    log_a = (jnp.minimum(x, 0.0) - jnp.log(1.0 + jnp.exp(-jnp.abs(x)))) * (1.0 / GLA_GATE_TEMP)

    row = lax.broadcasted_iota(jnp.int32, (c, c), 0)
    col = lax.broadcasted_iota(jnp.int32, (c, c), 1)
    keep = (col >= row) if reverse else (col <= row)
    ones = jnp.where(keep, 1.0, 0.0).astype(F32)
    cum = jnp.dot(ones, log_a, preferred_element_type=F32, precision=lax.Precision.HIGHEST)
    edge = 0 if reverse else c - 1
    total = cum[edge:edge + 1, :]
    mid = cum[c // 2:c // 2 + 1, :]

    q = q_ref[...].astype(F32) * (GLA_KEY_DIM ** -0.5)
    k = k_ref[...].astype(F32)
    q_in = (q * jnp.exp(cum - mid)).astype(BF16)
    k_in = (k * jnp.exp(mid - cum)).astype(BF16)
    q_st = (q * jnp.exp(cum)).astype(BF16)
    k_st = (k * jnp.exp(total - cum)).astype(BF16)
    dec = jnp.exp(total)

    for h in range(GLA_HEADS):
        ks = slice(h * GLA_KEY_DIM, (h + 1) * GLA_KEY_DIM)
        vs = slice(h * GLA_VAL_DIM, (h + 1) * GLA_VAL_DIM)
        vh = v_ref[:, vs]
        s = lax.dot_general(q_in[:, ks], k_in[:, ks], (((1,), (1,)), ((), ())),
                            preferred_element_type=F32)
        p = jnp.where(keep, s, 0.0).astype(BF16)
        st = st_ref[h]
        o = (jnp.dot(p, vh, preferred_element_type=F32)
             + lax.dot_general(q_st[:, ks], st.astype(BF16), (((1,), (1,)), ((), ())),
                               preferred_element_type=F32))
        o_ref[:, vs] = o.astype(o_ref.dtype)
        upd = lax.dot_general(vh, k_st[:, ks], (((0,), (0,)), ((), ())),
                              preferred_element_type=F32)
        st_ref[h] = st * dec[:, ks] + upd


def _gla_direction(main, z, gate_up, gate_bias, reverse):
    c = GLA_CHUNK
    nc = SEQ // c

    def rows(b, j):
        return b * nc + (nc - 1 - j if reverse else j)

    q_blk = FNET_WIDTH // GLA_QK
    k_blk = q_blk + 1
    v_blk = (FNET_WIDTH + 2 * GLA_QK) // GLA_V
    return pl.pallas_call(
        functools.partial(_gla_kernel, reverse=reverse),
        grid=(BATCH, nc),
        in_specs=[
            pl.BlockSpec((c, GLA_QK), lambda b, j: (rows(b, j), q_blk)),
            pl.BlockSpec((c, GLA_QK), lambda b, j: (rows(b, j), k_blk)),
            pl.BlockSpec((c, GLA_V), lambda b, j: (rows(b, j), v_blk)),
            pl.BlockSpec((c, 2 * GLA_GATE_RANK), lambda b, j: (rows(b, j), 0)),
            pl.BlockSpec((GLA_GATE_RANK, GLA_QK), lambda b, j: (0, 0)),
            pl.BlockSpec((1, GLA_QK), lambda b, j: (0, 0)),
        ],
        out_specs=pl.BlockSpec((c, GLA_V), lambda b, j: (rows(b, j), 0)),
        out_shape=jax.ShapeDtypeStruct((N_TOK, GLA_V), BF16),
        scratch_shapes=[pltpu.VMEM((GLA_HEADS, GLA_VAL_DIM, GLA_KEY_DIM), F32)],
        compiler_params=_cparams(("parallel", "arbitrary"), 32),
        name="gla_bwd" if reverse else "gla_fwd",
    )(main, main, main, z, gate_up, gate_bias)


def _out_proj_even_kernel(af_ref, of_ref, ob_ref, gt_ref, gg_ref, h_ref, w_ref, o_ref, lhs_ref):
    @pl.when(pl.program_id(1) == 0)
    def _():
        for dd in range(TM // FFT_R):
            lhs_ref[dd * FFT_R:(dd + 1) * FFT_R, :FNET_WIDTH] = (
                af_ref[0, :, dd * FNET_WIDTH:(dd + 1) * FNET_WIDTH])
        o = of_ref[...].astype(F32) + ob_ref[...].astype(F32)
        gt = gt_ref[...].astype(F32)
        for hh in range(GLA_HEADS):
            sl = slice(hh * GLA_VAL_DIM, (hh + 1) * GLA_VAL_DIM)
            y = _rms(o[:, sl], gg_ref[...])
            gv = gt[:, sl]
            y = y * (gv / (1.0 + jnp.exp(-gv)))
            lhs_ref[:, FNET_WIDTH + hh * GLA_VAL_DIM:FNET_WIDTH + (hh + 1) * GLA_VAL_DIM] = (
                y.astype(BF16))

    o_ref[...] = h_ref[...] + jnp.dot(lhs_ref[...], w_ref[...], preferred_element_type=F32)


def _out_proj_even(af, o_f, o_b, main, gla_g, h, w_out):
    tn = 1024
    tps = SEQ // TM
    t = TM // FFT_R
    af2 = af.reshape(BATCH, FFT_R, FFT_R * FNET_WIDTH)
    g_blk = (FNET_WIDTH + 2 * GLA_QK + GLA_V) // GLA_V
    return pl.pallas_call(
        _out_proj_even_kernel,
        grid=(N_TOK // TM, D_MODEL // tn),
        in_specs=[
            pl.BlockSpec((1, FFT_R, t * FNET_WIDTH), lambda i, j: (i // tps, 0, i % tps)),
            pl.BlockSpec((TM, GLA_V), lambda i, j: (i, 0)),
            pl.BlockSpec((TM, GLA_V), lambda i, j: (i, 0)),
            pl.BlockSpec((TM, GLA_V), lambda i, j: (i, g_blk)),
            pl.BlockSpec((1, GLA_VAL_DIM), lambda i, j: (0, 0)),
            pl.BlockSpec((TM, tn), lambda i, j: (i, j)),
            pl.BlockSpec((D_MODEL, tn), lambda i, j: (0, j)),
        ],
        out_specs=pl.BlockSpec((TM, tn), lambda i, j: (i, j)),
        out_shape=jax.ShapeDtypeStruct((N_TOK, D_MODEL), F32),
        scratch_shapes=[pltpu.VMEM((TM, D_MODEL), BF16)],
        compiler_params=_cparams(("parallel", "arbitrary"), 40),
        name="out_proj_even",
    )(af2, o_f, o_b, main, gla_g, h, w_out)


def _qkv_proj_kernel(h_ref, g_ref, w_ref, hg_ref, o_ref, xn_ref, *, n_normed, tn):
    j = pl.program_id(1)

    @pl.when(j == 0)
    def _():
        xn_ref[...] = _rms(h_ref[...], g_ref[...]).astype(BF16)

    acc = jnp.dot(xn_ref[...], w_ref[...], preferred_element_type=F32)

    @pl.when(j < n_normed)
    def _():
        for hh in range(tn // ATT_HEAD_DIM):
            sl = slice(hh * ATT_HEAD_DIM, (hh + 1) * ATT_HEAD_DIM)
            o_ref[:, sl] = _rms(acc[:, sl], hg_ref[:, sl]).astype(o_ref.dtype)

    @pl.when(j >= n_normed)
    def _():
        o_ref[...] = acc.astype(o_ref.dtype)


def _qkv_proj(h, g, w_qkv, head_gain):
    tn = 1024
    return pl.pallas_call(
        functools.partial(_qkv_proj_kernel, n_normed=2 * ATT_WIDTH // tn, tn=tn),
        grid=(N_TOK // TM, 3 * ATT_WIDTH // tn),
        in_specs=[
            pl.BlockSpec((TM, D_MODEL), lambda i, j: (i, 0)),
            pl.BlockSpec((1, D_MODEL), lambda i, j: (0, 0)),
            pl.BlockSpec((D_MODEL, tn), lambda i, j: (0, j)),
            pl.BlockSpec((1, tn), lambda i, j: (0, j)),
        ],
        out_specs=pl.BlockSpec((TM, tn), lambda i, j: (i, j)),
        out_shape=jax.ShapeDtypeStruct((N_TOK, 3 * ATT_WIDTH), BF16),
        scratch_shapes=[pltpu.VMEM((TM, D_MODEL), BF16)],
        compiler_params=_cparams(("parallel", "arbitrary"), 40),
        name="qkv_proj_odd",
    )(h, g, w_qkv, head_gain)


def _t5_bucket(rel):
    nb = REL_BUCKETS // 2
    ret = (rel > 0).astype(np.int32) * nb
    n = np.abs(rel)
    max_exact = nb // 2
    large = max_exact + (np.log(np.maximum(n, 1) / max_exact)
                         / np.log(REL_MAX_DISTANCE / max_exact)
                         * (nb - max_exact)).astype(np.int32)
    large = np.minimum(large, nb - 1)
    return (ret + np.where(n < max_exact, n, large)).astype(np.int32)


def _band_bias(rel_bias, dilation):
    kb = ATT_BLOCK + 2 * ATT_HALF
    rel = np.arange(kb)[None, :] - ATT_HALF - np.arange(ATT_BLOCK)[:, None]
    in_band = np.abs(rel) <= ATT_HALF
    bias = rel_bias[_t5_bucket(rel * dilation)].astype(F32)
    bias = jnp.where(in_band[:, :, None], bias, NEG)
    return bias.transpose(2, 0, 1)


def _dilated_kernel(q_ref, kp_ref, kc_ref, kn_ref, vp_ref, vc_ref, vn_ref, bias_ref,
                    o_ref, lse_ref, *, sub_len):
    n = pl.program_id(2)
    kb = ATT_BLOCK + 2 * ATT_HALF
    kpos = n * ATT_BLOCK - ATT_HALF + lax.broadcasted_iota(jnp.int32, (1, kb), 1)
    valid = (kpos >= 0) & (kpos < sub_len)
    lane = lax.broadcasted_iota(jnp.int32, (ATT_BLOCK, ATT_HEADS), 1)
    lse_all = jnp.zeros((ATT_BLOCK, ATT_HEADS), F32)
    for h in range(ATT_HEADS):
        sl = slice(h * ATT_HEAD_DIM, (h + 1) * ATT_HEAD_DIM)
        q = q_ref[0, :, sl]
        k = jnp.concatenate([kp_ref[0, ATT_HALF:, sl], kc_ref[0, :, sl], kn_ref[0, :ATT_HALF, sl]],
                            axis=0)
        v = jnp.concatenate([vp_ref[0, ATT_HALF:, sl], vc_ref[0, :, sl], vn_ref[0, :ATT_HALF, sl]],
                            axis=0)
        s = lax.dot_general(q, k, (((1,), (1,)), ((), ())), preferred_element_type=F32)
        s = jnp.where(valid, s + bias_ref[h], NEG)
        m = jnp.max(s, axis=-1, keepdims=True)
        p = jnp.exp(s - m)
        l = jnp.sum(p, axis=-1, keepdims=True)
        num = jnp.dot(p.astype(BF16), v, preferred_element_type=F32)
        o_ref[0, :, sl] = (num / l).astype(o_ref.dtype)
        lse_all = jnp.where(lane == h, m + jnp.log(l), lse_all)
    lse_ref[0, 0] = lse_all


def _dilated_branch(qkv, bias, dilation):
    d = dilation
    sub_len = SEQ // d
    nb = sub_len // ATT_BLOCK
    x = qkv.reshape(BATCH, sub_len, d * 3 * ATT_WIDTH)
    blk = (1, ATT_BLOCK, ATT_WIDTH)

    def spec(part, shift):
        def imap(b, r, n):
            return (b, jnp.clip(n + shift, 0, nb - 1), r * 3 + part)
        return pl.BlockSpec(blk, imap)

    out, lse = pl.pallas_call(
        functools.partial(_dilated_kernel, sub_len=sub_len),
        grid=(BATCH, d, nb),
        in_specs=[
            spec(0, 0),
            spec(1, -1), spec(1, 0), spec(1, 1),
            spec(2, -1), spec(2, 0), spec(2, 1),
            pl.BlockSpec((ATT_HEADS, ATT_BLOCK, ATT_BLOCK + 2 * ATT_HALF), lambda b, r, n: (0, 0, 0)),
        ],
        out_specs=[
            pl.BlockSpec(blk, lambda b, r, n: (b, n, r)),
            pl.BlockSpec((1, 1, ATT_BLOCK, ATT_HEADS), lambda b, r, n: (b, r, n, 0)),
        ],
        out_shape=[
            jax.ShapeDtypeStruct((BATCH, sub_len, d * ATT_WIDTH), BF16),
            jax.ShapeDtypeStruct((BATCH, d, sub_len, ATT_HEADS), F32),
        ],
        compiler_params=_cparams(("parallel", "parallel", "arbitrary"), 40),
        name=f"dilated_attn_d{d}",
    )(x, x, x, x, x, x, x, bias)
    return out.reshape(N_TOK, ATT_WIDTH), lse


def _out_proj_odd_kernel(o1_ref, o2_ref, o3_ref, l1_ref, l2_ref, l3_ref, h_ref, w_ref, o_ref,
                         lhs_ref):
    @pl.when(pl.program_id(1) == 0)
    def _():
        l1, l2, l3 = l1_ref[...], l2_ref[...], l3_ref[...]
        m = jnp.maximum(jnp.maximum(l1, l2), l3)
        e1, e2, e3 = jnp.exp(l1 - m), jnp.exp(l2 - m), jnp.exp(l3 - m)
        inv = 1.0 / (e1 + e2 + e3)
        w1, w2, w3 = e1 * inv, e2 * inv, e3 * inv
        for hh in range(ATT_HEADS):
            sl = slice(hh * ATT_HEAD_DIM, (hh + 1) * ATT_HEAD_DIM)
            acc = (w1[:, hh:hh + 1] * o1_ref[:, sl].astype(F32)
                   + w2[:, hh:hh + 1] * o2_ref[:, sl].astype(F32)
                   + w3[:, hh:hh + 1] * o3_ref[:, sl].astype(F32))
            lhs_ref[:, sl] = acc.astype(BF16)

    o_ref[...] = h_ref[...] + jnp.dot(lhs_ref[...], w_ref[...], preferred_element_type=F32)


def _out_proj_odd(outs, lses, h, w_out):
    tn = 1024
    o_spec = pl.BlockSpec((TM, ATT_WIDTH), lambda i, j: (i, 0))
    l_spec = pl.BlockSpec((TM, ATT_HEADS), lambda i, j: (i, 0))
    return pl.pallas_call(
        _out_proj_odd_kernel,
        grid=(N_TOK // TM, D_MODEL // tn),
        in_specs=[o_spec, o_spec, o_spec, l_spec, l_spec, l_spec,
                  pl.BlockSpec((TM, tn), lambda i, j: (i, j)),
                  pl.BlockSpec((ATT_WIDTH, tn), lambda i, j: (0, j))],
        out_specs=pl.BlockSpec((TM, tn), lambda i, j: (i, j)),
        out_shape=jax.ShapeDtypeStruct((N_TOK, D_MODEL), F32),
        scratch_shapes=[pltpu.VMEM((TM, ATT_WIDTH), BF16)],
        compiler_params=_cparams(("parallel", "arbitrary"), 48),
        name="out_proj_odd",
    )(*outs, *lses, h, w_out)


def _gelu_tanh(x):
    return 0.5 * x * (1.0 + jnp.tanh(0.7978845608028654 * (x + 0.044715 * (x * x * x))))


def _ffn_kernel(hp_ref, h_ref, hn_ref, g_ref, wg_ref, wu_ref, cw_ref, cb_ref, wd_ref, o_ref,
                xn_ref, gs_ref):
    i = pl.program_id(0)
    j = pl.program_id(1)
    tps = SEQ // TM

    @pl.when(j == 0)
    def _():
        gn = g_ref[...]
        pos = i % tps
        xp = jnp.where(pos == 0, 0.0, _rms(hp_ref[...], gn))
        xnx = jnp.where(pos == tps - 1, 0.0, _rms(hn_ref[...], gn))
        xn_ref[0:HALO, :] = xp.astype(BF16)
        xn_ref[HALO:HALO + TM, :] = _rms(h_ref[...], gn).astype(BF16)
        xn_ref[HALO + TM:, :] = xnx.astype(BF16)

    gs_ref[...] = jnp.dot(xn_ref[...], wg_ref[...], preferred_element_type=F32)
    up = jnp.dot(xn_ref[HALO:HALO + TM, :], wu_ref[...], preferred_element_type=F32)
    cw = cw_ref[...]
    gc = (cw[0:1] * gs_ref[HALO - 1:HALO - 1 + TM, :]
          + cw[1:2] * gs_ref[HALO:HALO + TM, :]
          + cw[2:3] * gs_ref[HALO + 1:HALO + 1 + TM, :]
          + cb_ref[...])
    act = (_gelu_tanh(gc) * up).astype(BF16)
    contrib = jnp.dot(act, wd_ref[...], preferred_element_type=F32)

    @pl.when(j == 0)
    def _():
        o_ref[...] = h_ref[...] + contrib

    @pl.when(j > 0)
    def _():
        o_ref[...] += contrib


def _conv_ffn(h, g, w_gate, w_up, conv_w, conv_b, w_down):
    tf = FFN_TF
    hb = TM // HALO
    n_halo_blocks = N_TOK // HALO
    return pl.pallas_call(
        _ffn_kernel,
        grid=(N_TOK // TM, D_FF // tf),
        in_specs=[
            pl.BlockSpec((HALO, D_MODEL), lambda i, j: (jnp.maximum(i * hb - 1, 0), 0)),
            pl.BlockSpec((TM, D_MODEL), lambda i, j: (i, 0)),
            pl.BlockSpec((HALO, D_MODEL),
                         lambda i, j: (jnp.minimum((i + 1) * hb, n_halo_blocks - 1), 0)),
            pl.BlockSpec((1, D_MODEL), lambda i, j: (0, 0)),
            pl.BlockSpec((D_MODEL, tf), lambda i, j: (0, j)),
            pl.BlockSpec((D_MODEL, tf), lambda i, j: (0, j)),
            pl.BlockSpec((3, tf), lambda i, j: (0, j)),
            pl.BlockSpec((1, tf), lambda i, j: (0, j)),
            pl.BlockSpec((tf, D_MODEL), lambda i, j: (j, 0)),
        ],
        out_specs=pl.BlockSpec((TM, D_MODEL), lambda i, j: (i, 0)),
        out_shape=jax.ShapeDtypeStruct((N_TOK, D_MODEL), F32),
        scratch_shapes=[pltpu.VMEM((TM + 2 * HALO, D_MODEL), BF16),
                        pltpu.VMEM((TM + 2 * HALO, tf), F32)],
        compiler_params=_cparams(("parallel", "arbitrary"), 48),
        name="conv_ffn",
    )(h, h, h, g, w_gate, w_up, conv_w, conv_b, w_down)


def kernel(x, mix_norm_g, w_in_even, gate_up_fwd, gate_bias_fwd, gate_up_bwd, gate_bias_bwd,
           gla_norm_g, w_out_even, w_qkv_odd, q_norm_g, k_norm_g, rel_bias, w_out_odd,
           ffn_norm_g, w_gate, w_up, conv_w, conv_b, w_down):
    h = x.reshape(N_TOK, D_MODEL)
    tables = _fft_tables()
    biases = [_band_bias(rel_bias, d) for (_, d) in DILATED_GROUPS]
    for layer in range(DEPTH):
        i = layer // 2
        g_mix = mix_norm_g[layer].reshape(1, D_MODEL)
        if layer % 2 == 0:
            w_in = w_in_even[i]
            main, z = _in_proj_even(h, g_mix, w_in[:, :EVEN_MAIN].astype(BF16),
                                    w_in[:, EVEN_MAIN:].astype(BF16))
            af = _fourier_mix(main[:, :FNET_WIDTH], tables)
            o_f = _gla_direction(main, z, gate_up_fwd[i], gate_bias_fwd[i].reshape(1, GLA_QK), False)
            o_b = _gla_direction(main, z, gate_up_bwd[i], gate_bias_bwd[i].reshape(1, GLA_QK), True)
            h = _out_proj_even(af, o_f, o_b, main, gla_norm_g[i].reshape(1, GLA_VAL_DIM), h,
                               w_out_even[i].astype(BF16))
        else:
            head_gain = jnp.concatenate([
                jnp.tile(q_norm_g[i] * (ATT_HEAD_DIM ** -0.5), ATT_HEADS),
                jnp.tile(k_norm_g[i], ATT_HEADS),
                jnp.ones((ATT_WIDTH,), F32)]).reshape(1, 3 * ATT_WIDTH)
            qkv = _qkv_proj(h, g_mix, w_qkv_odd[i].astype(BF16), head_gain)
            outs, lses = [], []
            for (_, d), bias in zip(DILATED_GROUPS, biases):
                o, lse = _dilated_branch(qkv, bias, d)
                outs.append(o)
                lses.append(lse.transpose(0, 2, 1, 3).reshape(N_TOK, ATT_HEADS))
            h = _out_proj_odd(outs, lses, h, w_out_odd[i].astype(BF16))
        h = _conv_ffn(h, ffn_norm_g[layer].reshape(1, D_MODEL), w_gate[layer].astype(BF16),
                      w_up[layer].astype(BF16), conv_w[layer], conv_b[layer].reshape(1, D_FF),
                      w_down[layer].astype(BF16))
    return h.reshape(BATCH, SEQ, D_MODEL)
```

```python
import functools

import numpy as np
import jax
import jax.numpy as jnp
from jax import lax
from jax.experimental import pallas as pl
from jax.experimental.pallas import tpu as pltpu

F32 = jnp.float32
BF16 = jnp.bfloat16

D_MODEL = 2048
BATCH = 4
SEQ = 4096
DEPTH = 4
N_TOK = BATCH * SEQ
FNET_GROUPS = 4
FNET_GROUP_DIM = 256
FNET_WIDTH = 1024
GLA_HEADS = 4
GLA_KEY_DIM = 128
GLA_VAL_DIM = 256
GLA_QK = 512
GLA_V = 1024
GLA_GATE_RANK = 16
GLA_GATE_TEMP = 16.0
EVEN_MAIN = FNET_WIDTH + 2 * GLA_QK + 2 * GLA_V
ATT_HEADS = 16
ATT_HEAD_DIM = 128
ATT_WIDTH = 2048
DILATED_GROUPS = ((128, 1), (512, 4), (2048, 16))
ATT_BLOCK = 128
ATT_HALF = 64
REL_BUCKETS = 32
REL_MAX_DISTANCE = 1024
D_FF = 5632
EPS = 1e-6
NEG = -1e30

LANES = 128
BF16_SUBLANES = 16
MIB = 1024 * 1024

TM = 512
FFT_R = 64
GLA_CHUNK = 64
FFN_TF = 512
HALO = BF16_SUBLANES


def _cparams(sem, vmem_mib):
    return pltpu.CompilerParams(dimension_semantics=sem, vmem_limit_bytes=vmem_mib * MIB)


def _rms(x, g):
    ms = jnp.mean(x * x, axis=-1, keepdims=True)
    return x * lax.rsqrt(ms + EPS) * g


def _slab_scratch(rows, width):
    return pltpu.VMEM((width // LANES, rows, LANES), F32)


def _slab_put(ref, x):
    for c in range(ref.shape[0]):
        ref[c] = x[:, c * LANES:(c + 1) * LANES]


def _slab_get(ref):
    return jnp.concatenate([ref[c] for c in range(ref.shape[0])], axis=1)


def _slab_put_rows(ref, start, size, stride, x):
    for c in range(ref.shape[0]):
        ref[c, pl.ds(start, size, stride=stride), :] = x[:, c * LANES:(c + 1) * LANES]


def _slab_get_rows(ref, start, size, stride):
    return jnp.concatenate(
        [ref[c, pl.ds(start, size, stride=stride), :] for c in range(ref.shape[0])], axis=1)


def _in_proj_even_kernel(h_ref, g_ref, w_ref, wz_ref, o_ref, z_ref, xn_ref):
    @pl.when(pl.program_id(1) == 0)
    def _():
        xn = _rms(h_ref[...], g_ref[...]).astype(BF16)
        xn_ref[...] = xn
        z_ref[...] = jnp.dot(xn, wz_ref[...], preferred_element_type=F32)

    o_ref[...] = jnp.dot(xn_ref[...], w_ref[...], preferred_element_type=F32).astype(o_ref.dtype)


def _in_proj_even(h, g, w_main, w_z):
    tn = 1024
    nz = w_z.shape[1]
    return pl.pallas_call(
        _in_proj_even_kernel,
        grid=(N_TOK // TM, EVEN_MAIN // tn),
        in_specs=[
            pl.BlockSpec((TM, D_MODEL), lambda i, j: (i, 0)),
            pl.BlockSpec((1, D_MODEL), lambda i, j: (0, 0)),
            pl.BlockSpec((D_MODEL, tn), lambda i, j: (0, j)),
            pl.BlockSpec((D_MODEL, nz), lambda i, j: (0, 0)),
        ],
        out_specs=[
            pl.BlockSpec((TM, tn), lambda i, j: (i, j)),
            pl.BlockSpec((TM, nz), lambda i, j: (i, 0)),
        ],
        out_shape=[
            jax.ShapeDtypeStruct((N_TOK, EVEN_MAIN), BF16),
            jax.ShapeDtypeStruct((N_TOK, nz), F32),
        ],
        scratch_shapes=[pltpu.VMEM((TM, D_MODEL), BF16)],
        compiler_params=_cparams(("parallel", "arbitrary"), 40),
        name="in_proj_even",
    )(h, g, w_main, w_z)


def _fft_tables():
    r = FFT_R
    idx = np.arange(r)
    ang = 2.0 * np.pi * np.outer(idx, idx) / r
    c, s = np.cos(ang), np.sin(ang)
    fa = np.concatenate([c, -s], axis=0)
    fb = np.block([[c, s], [-s, c]])
    tw = 2.0 * np.pi * np.outer(idx, idx) / (r * r)
    tr = np.repeat(np.cos(tw), LANES, axis=1)
    ti = np.repeat(-np.sin(tw), LANES, axis=1)
    cidx = np.arange(FNET_GROUP_DIM)
    cang = 2.0 * np.pi * np.outer(cidx, cidx) / FNET_GROUP_DIM
    scale = 1.0 / np.sqrt(SEQ * FNET_GROUP_DIM)
    cc = np.cos(cang) * scale
    sc = np.sin(cang) * scale
    return (jnp.asarray(fa, BF16), jnp.asarray(fb, BF16), jnp.asarray(tr, F32),
            jnp.asarray(ti, F32), jnp.asarray(cc, BF16), jnp.asarray(sc, BF16))


FFT_T = 16


def _fft_a_kernel(x_ref, fa_ref, tr_ref, ti_ref, zr_ref, zi_ref, xs_ref, zrs_ref, zis_ref):
    t = FFT_T
    _slab_put(xs_ref, x_ref[0].reshape(FFT_R * t, FNET_WIDTH).astype(F32))
    rep = FNET_WIDTH // LANES
    for s in range(t):
        xs = _slab_get_rows(xs_ref, s, FFT_R, t).astype(BF16)
        z = jnp.dot(fa_ref[...], xs, preferred_element_type=F32)
        tr = jnp.tile(tr_ref[:, s * LANES:(s + 1) * LANES], (1, rep))
        ti = jnp.tile(ti_ref[:, s * LANES:(s + 1) * LANES], (1, rep))
        a = z[:FFT_R]
        b = z[FFT_R:]
        _slab_put_rows(zrs_ref, s, FFT_R, t, a * tr - b * ti)
        _slab_put_rows(zis_ref, s, FFT_R, t, a * ti + b * tr)
    zr_ref[0] = _slab_get(zrs_ref).astype(BF16).reshape(FFT_R, t, FNET_WIDTH)
    zi_ref[0] = _slab_get(zis_ref).astype(BF16).reshape(FFT_R, t, FNET_WIDTH)


def _fft_b_kernel(zr_ref, zi_ref, fb_ref, cc_ref, sc_ref, o_ref, ur_ref, ui_ref, ys_ref):
    t = FFT_T
    for k1 in range(t):
        rhs = jnp.concatenate([zr_ref[0, k1], zi_ref[0, k1]], axis=0)
        u = jnp.dot(fb_ref[...], rhs, preferred_element_type=F32)
        ur_ref[k1 * FFT_R:(k1 + 1) * FFT_R, :] = u[:FFT_R].astype(BF16)
        ui_ref[k1 * FFT_R:(k1 + 1) * FFT_R, :] = u[FFT_R:].astype(BF16)
    for g in range(FNET_GROUPS):
        sl = slice(g * FNET_GROUP_DIM, (g + 1) * FNET_GROUP_DIM)
        y = (jnp.dot(ur_ref[:, sl], cc_ref[...], preferred_element_type=F32)
             + jnp.dot(ui_ref[:, sl], sc_ref[...], preferred_element_type=F32))
        for k1 in range(t):
            for c in range(FNET_GROUP_DIM // LANES):
                ys_ref[g * (FNET_GROUP_DIM // LANES) + c, pl.ds(k1, FFT_R, stride=t), :] = (
                    y[k1 * FFT_R:(k1 + 1) * FFT_R, c * LANES:(c + 1) * LANES])
    o_ref[0] = _slab_get(ys_ref).astype(o_ref.dtype).reshape(FFT_R, t, FNET_WIDTH)


def _fourier_mix(main, tables):
    fa, fb, tr, ti, cc, sc = tables
    r = FFT_R
    t = FFT_T
    x = main.reshape(BATCH, r, r, EVEN_MAIN)
    blk = (1, r, t, FNET_WIDTH)
    zr, zi = pl.pallas_call(
        _fft_a_kernel,
        grid=(BATCH, r // t),
        in_specs=[
            pl.BlockSpec(blk, lambda b, j: (b, 0, j, 0)),
            pl.BlockSpec((2 * r, r), lambda b, j: (0, 0)),
            pl.BlockSpec((r, t * LANES), lambda b, j: (0, j)),
            pl.BlockSpec((r, t * LANES), lambda b, j: (0, j)),
        ],
        out_specs=[pl.BlockSpec(blk, lambda b, j: (b, 0, j, 0))] * 2,
        out_shape=[jax.ShapeDtypeStruct((BATCH, r, r, FNET_WIDTH), BF16)] * 2,
        scratch_shapes=[_slab_scratch(r * t, FNET_WIDTH)] * 3,
        compiler_params=_cparams(("parallel", "parallel"), 40),
        name="fft_stage_a",
    )(x, fa, tr, ti)
    af = pl.pallas_call(
        _fft_b_kernel,
        grid=(BATCH, r // t),
        in_specs=[
            pl.BlockSpec((1, t, r, FNET_WIDTH), lambda b, j: (b, j, 0, 0)),
            pl.BlockSpec((1, t, r, FNET_WIDTH), lambda b, j: (b, j, 0, 0)),
            pl.BlockSpec((2 * r, 2 * r), lambda b, j: (0, 0)),
            pl.BlockSpec((FNET_GROUP_DIM, FNET_GROUP_DIM), lambda b, j: (0, 0)),
            pl.BlockSpec((FNET_GROUP_DIM, FNET_GROUP_DIM), lambda b, j: (0, 0)),
        ],
        out_specs=pl.BlockSpec(blk, lambda b, j: (b, 0, j, 0)),
        out_shape=jax.ShapeDtypeStruct((BATCH, r, r, FNET_WIDTH), BF16),
        scratch_shapes=[pltpu.VMEM((t * r, FNET_WIDTH), BF16)] * 2
        + [_slab_scratch(r * t, FNET_WIDTH)],
        compiler_params=_cparams(("parallel", "parallel"), 40),
        name="fft_stage_b",
    )(zr, zi, fb, cc, sc)
    return af.reshape(N_TOK, FNET_WIDTH)


def _gla_kernel(q_ref, k_ref, v_ref, z_ref, gu_ref, gb_ref, o_ref, st_ref, *, reverse):
    c = GLA_CHUNK

    @pl.when(pl.program_id(1) == 0)
    def _():
        st_ref[...] = jnp.zeros_like(st_ref)

    z = z_ref[...]
    zd = z[:, GLA_GATE_RANK:] if reverse else z[:, :GLA_GATE_RANK]
    x = jnp.dot(zd, gu_ref[...], preferred_element_type=F32,
                precision=lax.Precision.HIGHEST) + gb_ref[...]
    log_a = (jnp.minimum(x, 0.0) - jnp.log(1.0 + jnp.exp(-jnp.abs(x)))) * (1.0 / GLA_GATE_TEMP)

    row = lax.broadcasted_iota(jnp.int32, (c, c), 0)
    col = lax.broadcasted_iota(jnp.int32, (c, c), 1)
    keep = (col >= row) if reverse else (col <= row)
    ones = jnp.where(keep, 1.0, 0.0).astype(F32)
    cum = jnp.dot(ones, log_a, preferred_element_type=F32, precision=lax.Precision.HIGHEST)
    edge = 0 if reverse else c - 1
    total = cum[edge:edge + 1, :]
    mid = cum[c // 2:c // 2 + 1, :]

    q = q_ref[...].astype(F32) * (GLA_KEY_DIM ** -0.5)
    k = k_ref[...].astype(F32)
    q_in = (q * jnp.exp(cum - mid)).astype(BF16)
    k_in = (k * jnp.exp(mid - cum)).astype(BF16)
    q_st = (q * jnp.exp(cum)).astype(BF16)
    k_st = (k * jnp.exp(total - cum)).astype(BF16)
    dec = jnp.exp(total)

    for h in range(GLA_HEADS):
        ks = slice(h * GLA_KEY_DIM, (h + 1) * GLA_KEY_DIM)
        vs = slice(h * GLA_VAL_DIM, (h + 1) * GLA_VAL_DIM)
        vh = v_ref[:, vs]
        s = lax.dot_general(q_in[:, ks], k_in[:, ks], (((1,), (1,)), ((), ())),
                            preferred_element_type=F32)
        p = jnp.where(keep, s, 0.0).astype(BF16)
        st = st_ref[h]
        o = (jnp.dot(p, vh, preferred_element_type=F32)
             + lax.dot_general(q_st[:, ks], st.astype(BF16), (((1,), (1,)), ((), ())),
                               preferred_element_type=F32))
        o_ref[:, vs] = o.astype(o_ref.dtype)
        upd = lax.dot_general(vh, k_st[:, ks], (((0,), (0,)), ((), ())),
                              preferred_element_type=F32)
        st_ref[h] = st * dec[:, ks] + upd


def _gla_direction(main, z, gate_up, gate_bias, reverse):
    c = GLA_CHUNK
    nc = SEQ // c

    def rows(b, j):
        return b * nc + (nc - 1 - j if reverse else j)

    q_blk = FNET_WIDTH // GLA_QK
    k_blk = q_blk + 1
    v_blk = (FNET_WIDTH + 2 * GLA_QK) // GLA_V
    return pl.pallas_call(
        functools.partial(_gla_kernel, reverse=reverse),
        grid=(BATCH, nc),
        in_specs=[
            pl.BlockSpec((c, GLA_QK), lambda b, j: (rows(b, j), q_blk)),
            pl.BlockSpec((c, GLA_QK), lambda b, j: (rows(b, j), k_blk)),
            pl.BlockSpec((c, GLA_V), lambda b, j: (rows(b, j), v_blk)),
            pl.BlockSpec((c, 2 * GLA_GATE_RANK), lambda b, j: (rows(b, j), 0)),
            pl.BlockSpec((GLA_GATE_RANK, GLA_QK), lambda b, j: (0, 0)),
            pl.BlockSpec((1, GLA_QK), lambda b, j: (0, 0)),
        ],
        out_specs=pl.BlockSpec((c, GLA_V), lambda b, j: (rows(b, j), 0)),
        out_shape=jax.ShapeDtypeStruct((N_TOK, GLA_V), BF16),
        scratch_shapes=[pltpu.VMEM((GLA_HEADS, GLA_VAL_DIM, GLA_KEY_DIM), F32)],
        compiler_params=_cparams(("parallel", "arbitrary"), 32),
        name="gla_bwd" if reverse else "gla_fwd",
    )(main, main, main, z, gate_up, gate_bias)


def _out_proj_even_kernel(af_ref, of_ref, ob_ref, gt_ref, gg_ref, h_ref, w_ref, o_ref, lhs_ref):
    @pl.when(pl.program_id(1) == 0)
    def _():
        lhs_ref[:, :FNET_WIDTH] = af_ref[...]
        o = of_ref[...].astype(F32) + ob_ref[...].astype(F32)
        gt = gt_ref[...].astype(F32)
        for hh in range(GLA_HEADS):
            sl = slice(hh * GLA_VAL_DIM, (hh + 1) * GLA_VAL_DIM)
            y = _rms(o[:, sl], gg_ref[...])
            gv = gt[:, sl]
            y = y * (gv / (1.0 + jnp.exp(-gv)))
            lhs_ref[:, FNET_WIDTH + hh * GLA_VAL_DIM:FNET_WIDTH + (hh + 1) * GLA_VAL_DIM] = (
                y.astype(BF16))

    o_ref[...] = h_ref[...] + jnp.dot(lhs_ref[...], w_ref[...], preferred_element_type=F32)


def _out_proj_even(af, o_f, o_b, main, gla_g, h, w_out):
    tn = 1024
    g_blk = (FNET_WIDTH + 2 * GLA_QK + GLA_V) // GLA_V
    return pl.pallas_call(
        _out_proj_even_kernel,
        grid=(N_TOK // TM, D_MODEL // tn),
        in_specs=[
            pl.BlockSpec((TM, FNET_WIDTH), lambda i, j: (i, 0)),
            pl.BlockSpec((TM, GLA_V), lambda i, j: (i, 0)),
            pl.BlockSpec((TM, GLA_V), lambda i, j: (i, 0)),
            pl.BlockSpec((TM, GLA_V), lambda i, j: (i, g_blk)),
            pl.BlockSpec((1, GLA_VAL_DIM), lambda i, j: (0, 0)),
            pl.BlockSpec((TM, tn), lambda i, j: (i, j)),
            pl.BlockSpec((D_MODEL, tn), lambda i, j: (0, j)),
        ],
        out_specs=pl.BlockSpec((TM, tn), lambda i, j: (i, j)),
        out_shape=jax.ShapeDtypeStruct((N_TOK, D_MODEL), F32),
        scratch_shapes=[pltpu.VMEM((TM, D_MODEL), BF16)],
        compiler_params=_cparams(("parallel", "arbitrary"), 40),
        name="out_proj_even",
    )(af, o_f, o_b, main, gla_g, h, w_out)


RES = 16


def _qkv_proj_kernel(h_ref, g_ref, w_ref, hg_ref, o_ref, og_ref, xn_ref, ys_ref, *, n_normed, tn):
    j = pl.program_id(1)

    @pl.when(j == 0)
    def _():
        xn_ref[...] = _rms(h_ref[...], g_ref[...]).astype(BF16)

    acc = jnp.dot(xn_ref[...], w_ref[...], preferred_element_type=F32)

    @pl.when(j < n_normed)
    def _():
        for hh in range(tn // ATT_HEAD_DIM):
            sl = slice(hh * ATT_HEAD_DIM, (hh + 1) * ATT_HEAD_DIM)
            ys_ref[hh] = _rms(acc[:, sl], hg_ref[:, sl])

    @pl.when(j >= n_normed)
    def _():
        _slab_put(ys_ref, acc)

    o_ref[...] = _slab_get(ys_ref).astype(o_ref.dtype)
    for r in range(RES):
        og_ref[0, r] = _slab_get_rows(ys_ref, r, TM // RES, RES).astype(og_ref.dtype)


def _qkv_proj(h, g, w_qkv, head_gain):
    tn = 1024
    tps = SEQ // TM
    return pl.pallas_call(
        functools.partial(_qkv_proj_kernel, n_normed=2 * ATT_WIDTH // tn, tn=tn),
        grid=(N_TOK // TM, 3 * ATT_WIDTH // tn),
        in_specs=[
            pl.BlockSpec((TM, D_MODEL), lambda i, j: (i, 0)),
            pl.BlockSpec((1, D_MODEL), lambda i, j: (0, 0)),
            pl.BlockSpec((D_MODEL, tn), lambda i, j: (0, j)),
            pl.BlockSpec((1, tn), lambda i, j: (0, j)),
        ],
        out_specs=[
            pl.BlockSpec((TM, tn), lambda i, j: (i, j)),
            pl.BlockSpec((1, RES, TM // RES, tn), lambda i, j: (i // tps, 0, i % tps, j)),
        ],
        out_shape=[
            jax.ShapeDtypeStruct((N_TOK, 3 * ATT_WIDTH), BF16),
            jax.ShapeDtypeStruct((BATCH, RES, SEQ // RES, 3 * ATT_WIDTH), BF16),
        ],
        scratch_shapes=[pltpu.VMEM((TM, D_MODEL), BF16), _slab_scratch(TM, tn)],
        compiler_params=_cparams(("parallel", "arbitrary"), 40),
        name="qkv_proj_odd",
    )(h, g, w_qkv, head_gain)


def _t5_bucket(rel):
    nb = REL_BUCKETS // 2
    ret = (rel > 0).astype(np.int32) * nb
    n = np.abs(rel)
    max_exact = nb // 2
    large = max_exact + (np.log(np.maximum(n, 1) / max_exact)
                         / np.log(REL_MAX_DISTANCE / max_exact)
                         * (nb - max_exact)).astype(np.int32)
    large = np.minimum(large, nb - 1)
    return (ret + np.where(n < max_exact, n, large)).astype(np.int32)


KEY_WIN = ATT_BLOCK + 2 * ATT_HALF

_BRANCH_GEOM = {1: (1, 1, SEQ), 4: (4, 4, SEQ // RES), 16: (1, RES, SEQ // RES)}


def _branch_positions(dilation):
    groups = _BRANCH_GEOM[dilation][0]
    run = ATT_BLOCK // groups
    c = np.arange(groups)[:, None]
    qpos = (groups * np.arange(run)[None, :] + c).reshape(-1)
    before = (groups * (np.arange(run // 2)[None, :] - run // 2) + c).reshape(-1)
    after = (groups * (run + np.arange(run // 2)[None, :]) + c).reshape(-1)
    return qpos, np.concatenate([before, qpos, after])


def _bias_kernel(ids_ref, rb_ref, o_ref):
    h = pl.program_id(1)
    ids = ids_ref[0]
    bias = jnp.full(ids.shape, NEG, F32)
    for b in range(REL_BUCKETS):
        bias = jnp.where(ids == b, rb_ref[b, h], bias)
    o_ref[0, 0] = bias


def _band_biases(rel_bias):
    ids = []
    for (_, d) in DILATED_GROUPS:
        qpos, kpos = _branch_positions(d)
        rel = kpos[None, :] - qpos[:, None]
        ids.append(np.where(np.abs(rel) <= ATT_HALF, _t5_bucket(rel * d), -1))
    ids = jnp.asarray(np.stack(ids), jnp.int32)
    nbr = len(DILATED_GROUPS)
    return pl.pallas_call(
        _bias_kernel,
        grid=(nbr, ATT_HEADS),
        in_specs=[
            pl.BlockSpec((1, ATT_BLOCK, KEY_WIN), lambda g, h: (g, 0, 0)),
            pl.BlockSpec(memory_space=pltpu.SMEM),
        ],
        out_specs=pl.BlockSpec((1, 1, ATT_BLOCK, KEY_WIN), lambda g, h: (g, h, 0, 0)),
        out_shape=jax.ShapeDtypeStruct((nbr, ATT_HEADS, ATT_BLOCK, KEY_WIN), F32),
        compiler_params=_cparams(("arbitrary", "arbitrary"), 16),
        name="band_bias",
    )(ids, rel_bias)


def _dilated_kernel(q_ref, kp_ref, kc_ref, kn_ref, vp_ref, vc_ref, vn_ref, bias_ref, kpos_ref,
                    o_ref, lse_ref, *, sub_len, groups):
    n = pl.program_id(2)
    run = ATT_BLOCK // groups
    half = run // 2
    kpos = n * ATT_BLOCK + kpos_ref[...]
    valid = (kpos >= 0) & (kpos < sub_len)
    lane = lax.broadcasted_iota(jnp.int32, (ATT_BLOCK, ATT_HEADS), 1)
    lse_all = jnp.zeros((ATT_BLOCK, ATT_HEADS), F32)

    def window(p_ref, c_ref, n_ref, sl):
        return jnp.concatenate([
            p_ref[0, :, 0, half:, sl].reshape(groups * half, ATT_HEAD_DIM),
            c_ref[0, :, 0, :, sl].reshape(ATT_BLOCK, ATT_HEAD_DIM),
            n_ref[0, :, 0, :half, sl].reshape(groups * half, ATT_HEAD_DIM)], axis=0)

    for h in range(ATT_HEADS):
        sl = slice(h * ATT_HEAD_DIM, (h + 1) * ATT_HEAD_DIM)
        q = q_ref[0, :, 0, :, sl].reshape(ATT_BLOCK, ATT_HEAD_DIM)
        k = window(kp_ref, kc_ref, kn_ref, sl)
        v = window(vp_ref, vc_ref, vn_ref, sl)
        s = lax.dot_general(q, k, (((1,), (1,)), ((), ())), preferred_element_type=F32)
        s = jnp.where(valid, s + bias_ref[0, h], NEG)
        m = jnp.max(s, axis=-1, keepdims=True)
        p = jnp.exp(s - m)
        l = jnp.sum(p, axis=-1, keepdims=True)
        num = jnp.dot(p.astype(BF16), v, preferred_element_type=F32)
        o_ref[0, :, 0, :, sl] = (num / l).astype(o_ref.dtype).reshape(groups, run, ATT_HEAD_DIM)
        lse_all = jnp.where(lane == h, m + jnp.log(l), lse_all)
    lse_ref[0, :, 0] = lse_all.reshape(groups, run, ATT_HEADS)


def _dilated_branch(qkv, biases, branch, dilation):
    groups, residues, rows = _BRANCH_GEOM[dilation]
    sub_len = SEQ // dilation
    run = ATT_BLOCK // groups
    nb = rows // run
    x = qkv.reshape(BATCH, groups, residues, rows, 3 * ATT_WIDTH)
    blk = (1, groups, 1, run, ATT_WIDTH)
    kpos = jnp.asarray(_branch_positions(dilation)[1].reshape(1, KEY_WIN), jnp.int32)

    def spec(part, shift):
        def imap(b, r, n):
            return (b, 0, r, jnp.clip(n + shift, 0, nb - 1), part)
        return pl.BlockSpec(blk, imap)

    return pl.pallas_call(
        functools.partial(_dilated_kernel, sub_len=sub_len, groups=groups),
        grid=(BATCH, residues, nb),
        in_specs=[
            spec(0, 0),
            spec(1, -1), spec(1, 0), spec(1, 1),
            spec(2, -1), spec(2, 0), spec(2, 1),
            pl.BlockSpec((1, ATT_HEADS, ATT_BLOCK, KEY_WIN), lambda b, r, n: (branch, 0, 0, 0)),
            pl.BlockSpec((1, KEY_WIN), lambda b, r, n: (0, 0)),
        ],
        out_specs=[
            pl.BlockSpec(blk, lambda b, r, n: (b, 0, r, n, 0)),
            pl.BlockSpec((1, groups, 1, run, ATT_HEADS), lambda b, r, n: (b, 0, r, n, 0)),
        ],
        out_shape=[
            jax.ShapeDtypeStruct((BATCH, groups, residues, rows, ATT_WIDTH), BF16),
            jax.ShapeDtypeStruct((BATCH, groups, residues, rows, ATT_HEADS), F32),
        ],
        compiler_params=_cparams(("parallel", "parallel", "arbitrary"), 40),
        name=f"dilated_attn_d{dilation}",
    )(x, x, x, x, x, x, x, biases, kpos)


def _out_proj_odd_kernel(o1_ref, o2_ref, o3_ref, l1_ref, l2_ref, l3_ref, ex_ref, h_ref, w_ref,
                         o_ref, lhs_ref, u2_ref, u3_ref):
    @pl.when(pl.program_id(1) == 0)
    def _():
        for r in range(RES):
            _slab_put_rows(u2_ref, r, TM // RES, RES, o2_ref[0, r].astype(F32))
            _slab_put_rows(u3_ref, r, TM // RES, RES, o3_ref[0, r].astype(F32))
        l1, l2, l3 = l1_ref[...], l2_ref[...], l3_ref[...]
        m = jnp.maximum(jnp.maximum(l1, l2), l3)
        e1, e2, e3 = jnp.exp(l1 - m), jnp.exp(l2 - m), jnp.exp(l3 - m)
        inv = 1.0 / (e1 + e2 + e3)
        w2 = jnp.dot((e2 * inv).astype(BF16), ex_ref[...], preferred_element_type=F32)
        w3 = jnp.dot((e3 * inv).astype(BF16), ex_ref[...], preferred_element_type=F32)
        o1 = o1_ref[...].astype(F32)
        lhs_ref[...] = (o1 + w2 * (_slab_get(u2_ref) - o1)
                        + w3 * (_slab_get(u3_ref) - o1)).astype(BF16)

    o_ref[...] = h_ref[...] + jnp.dot(lhs_ref[...], w_ref[...], preferred_element_type=F32)


def _out_proj_odd(o1, o2g, o3g, lses, h, w_out):
    tn = 1024
    tps = SEQ // TM
    expand = jnp.asarray(np.repeat(np.eye(ATT_HEADS), ATT_HEAD_DIM, axis=1), BF16)
    g_spec = pl.BlockSpec((1, RES, TM // RES, ATT_WIDTH), lambda i, j: (i // tps, 0, i % tps, 0))
    l_spec = pl.BlockSpec((TM, ATT_HEADS), lambda i, j: (i, 0))
    return pl.pallas_call(
        _out_proj_odd_kernel,
        grid=(N_TOK // TM, D_MODEL // tn),
        in_specs=[pl.BlockSpec((TM, ATT_WIDTH), lambda i, j: (i, 0)), g_spec, g_spec,
                  l_spec, l_spec, l_spec,
                  pl.BlockSpec((ATT_HEADS, ATT_WIDTH), lambda i, j: (0, 0)),
                  pl.BlockSpec((TM, tn), lambda i, j: (i, j)),
                  pl.BlockSpec((ATT_WIDTH, tn), lambda i, j: (0, j))],
        out_specs=pl.BlockSpec((TM, tn), lambda i, j: (i, j)),
        out_shape=jax.ShapeDtypeStruct((N_TOK, D_MODEL), F32),
        scratch_shapes=[pltpu.VMEM((TM, ATT_WIDTH), BF16), _slab_scratch(TM, ATT_WIDTH),
                        _slab_scratch(TM, ATT_WIDTH)],
        compiler_params=_cparams(("parallel", "arbitrary"), 56),
        name="out_proj_odd",
    )(o1, o2g, o3g, *lses, expand, h, w_out)


def _gelu_tanh(x):
    return 0.5 * x * (1.0 + jnp.tanh(0.7978845608028654 * (x + 0.044715 * (x * x * x))))


def _ffn_kernel(hp_ref, h_ref, hn_ref, g_ref, wg_ref, wu_ref, cw_ref, cb_ref, wd_ref, o_ref,
                xn_ref, gs_ref):
    i = pl.program_id(0)
    j = pl.program_id(1)
    tps = SEQ // TM

    @pl.when(j == 0)
    def _():
        gn = g_ref[...]
        pos = i % tps
        xp = jnp.where(pos == 0, 0.0, _rms(hp_ref[...], gn))
        xnx = jnp.where(pos == tps - 1, 0.0, _rms(hn_ref[...], gn))
        xn_ref[0:HALO, :] = xp.astype(BF16)
        xn_ref[HALO:HALO + TM, :] = _rms(h_ref[...], gn).astype(BF16)
        xn_ref[HALO + TM:, :] = xnx.astype(BF16)
        o_ref[...] = h_ref[...]

    gs_ref[...] = jnp.dot(xn_ref[...], wg_ref[...], preferred_element_type=F32)
    up = jnp.dot(xn_ref[HALO:HALO + TM, :], wu_ref[...], preferred_element_type=F32)
    cw = cw_ref[...]
    gc = (cw[0:1] * gs_ref[HALO - 1:HALO - 1 + TM, :]
          + cw[1:2] * gs_ref[HALO:HALO + TM, :]
          + cw[2:3] * gs_ref[HALO + 1:HALO + 1 + TM, :]
          + cb_ref[...])
    act = (_gelu_tanh(gc) * up).astype(BF16)
    o_ref[...] += jnp.dot(act, wd_ref[...], preferred_element_type=F32)


def _conv_ffn(h, g, w_gate, w_up, conv_w, conv_b, w_down):
    tf = FFN_TF
    hb = TM // HALO
    n_halo_blocks = N_TOK // HALO
    return pl.pallas_call(
        _ffn_kernel,
        grid=(N_TOK // TM, D_FF // tf),
        in_specs=[
            pl.BlockSpec((HALO, D_MODEL), lambda i, j: (jnp.maximum(i * hb - 1, 0), 0)),
            pl.BlockSpec((TM, D_MODEL), lambda i, j: (i, 0)),
            pl.BlockSpec((HALO, D_MODEL),
                         lambda i, j: (jnp.minimum((i + 1) * hb, n_halo_blocks - 1), 0)),
            pl.BlockSpec((1, D_MODEL), lambda i, j: (0, 0)),
            pl.BlockSpec((D_MODEL, tf), lambda i, j: (0, j)),
            pl.BlockSpec((D_MODEL, tf), lambda i, j: (0, j)),
            pl.BlockSpec((3, tf), lambda i, j: (0, j)),
            pl.BlockSpec((1, tf), lambda i, j: (0, j)),
            pl.BlockSpec((tf, D_MODEL), lambda i, j: (j, 0)),
        ],
        out_specs=pl.BlockSpec((TM, D_MODEL), lambda i, j: (i, 0)),
        out_shape=jax.ShapeDtypeStruct((N_TOK, D_MODEL), F32),
        scratch_shapes=[pltpu.VMEM((TM + 2 * HALO, D_MODEL), BF16),
                        pltpu.VMEM((TM + 2 * HALO, tf), F32)],
        compiler_params=_cparams(("parallel", "arbitrary"), 48),
        name="conv_ffn",
    )(h, h, h, g, w_gate, w_up, conv_w, conv_b, w_down)


def kernel(x, mix_norm_g, w_in_even, gate_up_fwd, gate_bias_fwd, gate_up_bwd, gate_bias_bwd,
           gla_norm_g, w_out_even, w_qkv_odd, q_norm_g, k_norm_g, rel_bias, w_out_odd,
           ffn_norm_g, w_gate, w_up, conv_w, conv_b, w_down):
    h = x.reshape(N_TOK, D_MODEL)
    tables = _fft_tables()
    biases = _band_biases(rel_bias)
    for layer in range(DEPTH):
        i = layer // 2
        g_mix = mix_norm_g[layer].reshape(1, D_MODEL)
        if layer % 2 == 0:
            w_in = w_in_even[i]
            main, z = _in_proj_even(h, g_mix, w_in[:, :EVEN_MAIN].astype(BF16),
                                    w_in[:, EVEN_MAIN:].astype(BF16))
            af = _fourier_mix(main, tables)
            o_f = _gla_direction(main, z, gate_up_fwd[i], gate_bias_fwd[i].reshape(1, GLA_QK), False)
            o_b = _gla_direction(main, z, gate_up_bwd[i], gate_bias_bwd[i].reshape(1, GLA_QK), True)
            h = _out_proj_even(af, o_f, o_b, main, gla_norm_g[i].reshape(1, GLA_VAL_DIM), h,
                               w_out_even[i].astype(BF16))
        else:
            head_gain = jnp.concatenate([
                jnp.tile(q_norm_g[i] * (ATT_HEAD_DIM ** -0.5), ATT_HEADS),
                jnp.tile(k_norm_g[i], ATT_HEADS),
                jnp.ones((ATT_WIDTH,), F32)]).reshape(1, 3 * ATT_WIDTH)
            qkv, qkv_g = _qkv_proj(h, g_mix, w_qkv_odd[i].astype(BF16), head_gain)
            o1, lse1 = _dilated_branch(qkv, biases, 0, 1)
            o2, lse2 = _dilated_branch(qkv_g, biases, 1, 4)
            o3, lse3 = _dilated_branch(qkv_g, biases, 2, 16)

            def natural(lse_g):
                return (lse_g.reshape(BATCH, RES, SEQ // RES, ATT_HEADS)
                        .transpose(0, 2, 1, 3).reshape(N_TOK, ATT_HEADS))

            h = _out_proj_odd(o1.reshape(N_TOK, ATT_WIDTH),
                              o2.reshape(BATCH, RES, SEQ // RES, ATT_WIDTH),
                              o3.reshape(BATCH, RES, SEQ // RES, ATT_WIDTH),
                              [lse1.reshape(N_TOK, ATT_HEADS), natural(lse2), natural(lse3)],
                              h, w_out_odd[i].astype(BF16))
        h = _conv_ffn(h, ffn_norm_g[layer].reshape(1, D_MODEL), w_gate[layer].astype(BF16),
                      w_up[layer].astype(BF16), conv_w[layer], conv_b[layer].reshape(1, D_FF),
                      w_down[layer].astype(BF16))
    return h.reshape(BATCH, SEQ, D_MODEL)
```
